```python
import jax, jax.numpy as jnp
from jax import lax
import numpy as np

D_MODEL = 2048
BATCH = 4
SEQ = 4096
DEPTH = 4

D_FF = 5632
POOL_WINDOWS = (2, 4, 8, 16)
N_POOL_GROUPS = len(POOL_WINDOWS)
POOL_WIDTH = 1024
POOL_GROUP_DIM = POOL_WIDTH // N_POOL_GROUPS
SGU_HEADS = 8
SGU_HEAD_DIM = 128
SGU_WIDTH = SGU_HEADS * SGU_HEAD_DIM
CHUNK = 128
IN_PROJ_WIDTH = POOL_WIDTH + 2 * SGU_WIDTH + 2 * D_MODEL
MACARON_WEIGHT = 0.5
EPS = 1e-6

kernel_name = "hybrid_pool_sgu_macaron_block"


def rmsnorm(x, g):
    xf = x.astype(jnp.float32)
    var = jnp.mean(xf * xf, axis=-1, keepdims=True)
    return (xf * lax.rsqrt(var + EPS)).astype(x.dtype) * g


def swiglu(h, w_up, w_down):
    gate, up = jnp.split(h @ w_up, 2, axis=-1)
    return (jax.nn.silu(gate) * up) @ w_down


def pool_mixer(p, w_group, scale):
    B, S, _ = p.shape
    maxw = POOL_WINDOWS[-1]
    pf = p.astype(jnp.float32)
    cs = jnp.cumsum(pf, axis=1)
    cs_pad = jnp.pad(cs, ((0, 0), (maxw, 0), (0, 0)))
    pos = jnp.arange(1, S + 1, dtype=jnp.int32)
    outs = []
    for g, w in enumerate(POOL_WINDOWS):
        sl = slice(g * POOL_GROUP_DIM, (g + 1) * POOL_GROUP_DIM)
        prev = cs_pad[:, maxw - w: maxw - w + S, sl]
        cnt = jnp.minimum(pos, w).astype(jnp.float32)[None, :, None]
        outs.append((cs[:, :, sl] - prev) / cnt - pf[:, :, sl])
    d = jnp.stack(outs, axis=2).astype(p.dtype)
    y = jnp.einsum('bsgc,gcd->bsgd', d, w_group)
    return y.reshape(B, S, POOL_WIDTH) * scale


def spatial_gating(u, v, v_gain, w_s, b_s):
    B, S, _ = u.shape
    n_chunks = S // CHUNK
    v = rmsnorm(v, v_gain)
    vc = v.reshape(B, n_chunks, CHUNK, SGU_HEADS, SGU_HEAD_DIM)
    w = w_s * jnp.tril(jnp.ones((CHUNK, CHUNK), dtype=w_s.dtype))
    s = jnp.einsum('hts,bnshc->bnthc', w, vc) + b_s.T[None, None, :, :, None]
    return u * s.reshape(B, S, SGU_WIDTH)


def setup_inputs(seed: int = 0) -> dict:
    key = jax.random.key(seed)
    ks = jax.random.split(key, 32)
    L, D = DEPTH, D_MODEL

    def dense(k, shape, fan_in):
        return jax.random.normal(k, shape, jnp.float32) * (fan_in ** -0.5)

    def gain(k, shape):
        return 1.0 + 0.05 * jax.random.normal(k, shape, jnp.float32)

    return {
        "x": jax.random.normal(ks[0], (BATCH, SEQ, D), jnp.float32),
        "g_ffn1_pre": gain(ks[1], (L, D)),
        "w_ffn1_up": dense(ks[2], (L, D, 2 * D_FF), D),
        "w_ffn1_down": dense(ks[3], (L, D_FF, D), D_FF),
        "g_ffn1_post": gain(ks[4], (L, D)),
        "g_mix_pre": gain(ks[5], (L, D)),
        "w_in": dense(ks[6], (L, D, IN_PROJ_WIDTH), D),
        "pool_group_w": dense(ks[7], (L, N_POOL_GROUPS, POOL_GROUP_DIM, POOL_GROUP_DIM), POOL_GROUP_DIM),
        "pool_scale": gain(ks[8], (L, POOL_WIDTH)),
        "w_pool_out": dense(ks[9], (L, POOL_WIDTH, D), POOL_WIDTH),
        "sgu_v_gain": gain(ks[10], (L, SGU_WIDTH)),
        "sgu_w_s": dense(ks[11], (L, SGU_HEADS, CHUNK, CHUNK), CHUNK),
        "sgu_b_s": gain(ks[12], (L, SGU_HEADS, CHUNK)),
        "w_sgu_out": dense(ks[13], (L, SGU_WIDTH, D), SGU_WIDTH),
        "w_out": dense(ks[14], (L, D, D), D),
        "g_mix_post": gain(ks[15], (L, D)),
        "g_ffn2_pre": gain(ks[16], (L, D)),
        "w_ffn2_up": dense(ks[17], (L, D, 2 * D_FF), D),
        "w_ffn2_down": dense(ks[18], (L, D_FF, D), D_FF),
        "g_ffn2_post": gain(ks[19], (L, D)),
    }


def reference(x, g_ffn1_pre, w_ffn1_up, w_ffn1_down, g_ffn1_post, g_mix_pre, w_in,
              pool_group_w, pool_scale, w_pool_out, sgu_v_gain, sgu_w_s, sgu_b_s,
              w_sgu_out, w_out, g_mix_post, g_ffn2_pre, w_ffn2_up, w_ffn2_down, g_ffn2_post):
    splits = (POOL_WIDTH, POOL_WIDTH + SGU_WIDTH, POOL_WIDTH + 2 * SGU_WIDTH,
              POOL_WIDTH + 2 * SGU_WIDTH + D_MODEL)
    for i in range(DEPTH):
        f = swiglu(rmsnorm(x, g_ffn1_pre[i]), w_ffn1_up[i], w_ffn1_down[i])
        x = x + MACARON_WEIGHT * rmsnorm(f, g_ffn1_post[i])

        h = rmsnorm(x, g_mix_pre[i])
        p, u, v, ga, gb = jnp.split(h @ w_in[i], splits, axis=-1)
        y_a = pool_mixer(p, pool_group_w[i], pool_scale[i]) @ w_pool_out[i]
        y_b = spatial_gating(jax.nn.gelu(u), jax.nn.gelu(v), sgu_v_gain[i],
                             sgu_w_s[i], sgu_b_s[i]) @ w_sgu_out[i]
        m = jax.nn.sigmoid(ga) * y_a + jax.nn.sigmoid(gb) * y_b
        x = x + rmsnorm(m @ w_out[i], g_mix_post[i])

        f = swiglu(rmsnorm(x, g_ffn2_pre[i]), w_ffn2_up[i], w_ffn2_down[i])
        x = x + MACARON_WEIGHT * rmsnorm(f, g_ffn2_post[i])
    return x
```

```python
import functools

import jax
import jax.numpy as jnp
from jax import lax
from jax.experimental import pallas as pl
from jax.experimental.pallas import tpu as pltpu

POOL_WINDOWS = (2, 4, 8, 16)
POOL_GROUP_DIM = 256
POOL_WIDTH = POOL_GROUP_DIM * len(POOL_WINDOWS)
POOL_HALO = 16
SGU_HEADS = 8
SGU_HEAD_DIM = 128
SGU_WIDTH = SGU_HEADS * SGU_HEAD_DIM
CHUNK = 128
MACARON_WEIGHT = 0.5
EPS = 1e-6

V7X_VMEM_LIMIT_BYTES = 56 * 1024 * 1024

BF16 = jnp.bfloat16
F32 = jnp.float32


def _rms_scale(x):
    var = jnp.mean(x * x, axis=-1, keepdims=True)
    return x * lax.rsqrt(var + EPS)


def _ffn_kernel(x_ref, gpre_ref, wg_ref, wu_ref, wd_ref, gpost_ref, o_ref, h_ref, acc_ref):
    k = pl.program_id(1)

    @pl.when(k == 0)
    def _():
        h_ref[...] = (_rms_scale(x_ref[...]) * gpre_ref[...]).astype(BF16)
        acc_ref[...] = jnp.zeros_like(acc_ref)

    h = h_ref[...]
    gate = jnp.dot(h, wg_ref[...], preferred_element_type=F32)
    up = jnp.dot(h, wu_ref[...], preferred_element_type=F32)
    a = (gate * jax.nn.sigmoid(gate) * up).astype(BF16)
    acc_ref[...] += jnp.dot(a, wd_ref[...], preferred_element_type=F32)

    @pl.when(k == pl.num_programs(1) - 1)
    def _():
        f = _rms_scale(acc_ref[...]) * gpost_ref[...]
        o_ref[...] = x_ref[...] + MACARON_WEIGHT * f


def _ffn(x, g_pre, w_up, w_down, g_post, layer, *, tm, tf):
    m, d = x.shape
    d_ff = w_down.shape[1]
    nk = d_ff // tf
    gain_spec = pl.BlockSpec((None, 1, d), lambda i, k: (layer, 0, 0))
    return pl.pallas_call(
        _ffn_kernel,
        grid=(m // tm, nk),
        in_specs=[
            pl.BlockSpec((tm, d), lambda i, k: (i, 0)),
            gain_spec,
            pl.BlockSpec((None, d, tf), lambda i, k: (layer, 0, k)),
            pl.BlockSpec((None, d, tf), lambda i, k: (layer, 0, k + nk)),
            pl.BlockSpec((None, tf, d), lambda i, k: (layer, k, 0)),
            gain_spec,
        ],
        out_specs=pl.BlockSpec((tm, d), lambda i, k: (i, 0)),
        out_shape=jax.ShapeDtypeStruct((m, d), F32),
        scratch_shapes=[pltpu.VMEM((tm, d), BF16), pltpu.VMEM((tm, d), F32)],
        compiler_params=pltpu.CompilerParams(
            dimension_semantics=("arbitrary", "arbitrary"),
            vmem_limit_bytes=V7X_VMEM_LIMIT_BYTES),
        name="ffn",
    )(x, g_pre, w_up, w_up, w_down, g_post)


def _inproj_kernel(x_ref, g_ref, w_ref, o_ref, h_ref):
    @pl.when(pl.program_id(1) == 0)
    def _():
        h_ref[...] = (_rms_scale(x_ref[...]) * g_ref[...]).astype(BF16)

    o_ref[...] = jnp.dot(h_ref[...], w_ref[...], preferred_element_type=F32)


def _inproj(x, g, w_in, layer, *, tm, tn):
    m, d = x.shape
    n = w_in.shape[2]
    return pl.pallas_call(
        _inproj_kernel,
        grid=(m // tm, n // tn),
        in_specs=[
            pl.BlockSpec((tm, d), lambda i, j: (i, 0)),
            pl.BlockSpec((None, 1, d), lambda i, j: (layer, 0, 0)),
            pl.BlockSpec((None, d, tn), lambda i, j: (layer, 0, j)),
        ],
        out_specs=pl.BlockSpec((tm, tn), lambda i, j: (i, j)),
        out_shape=jax.ShapeDtypeStruct((m, n), F32),
        scratch_shapes=[pltpu.VMEM((tm, d), BF16)],
        compiler_params=pltpu.CompilerParams(
            dimension_semantics=("arbitrary", "arbitrary"),
            vmem_limit_bytes=V7X_VMEM_LIMIT_BYTES),
        name="in_proj",
    )(x, g, w_in)


def _mixer_kernel(p_ref, u_ref, v_ref, ga0_ref, ga1_ref, gb0_ref, gb1_ref, x_ref,
                  wgrp_ref, pscale_ref, wpo_ref, vgain_ref, ws_ref, bst_ref,
                  wso_ref, wout_ref, gpost_ref, o_ref, pbuf_ref, d_ref, sg_ref,
                  *, tm, tiles_per_seq):
    i = pl.program_id(0)
    tile_in_seq = i % tiles_per_seq

    @pl.when(tile_in_seq == 0)
    def _():
        pbuf_ref[0:POOL_HALO, :] = jnp.zeros((POOL_HALO, POOL_WIDTH), F32)

    @pl.when(tile_in_seq != 0)
    def _():
        pbuf_ref[0:POOL_HALO, :] = pbuf_ref[tm:tm + POOL_HALO, :]

    pbuf_ref[POOL_HALO:POOL_HALO + tm, :] = p_ref[...]

    pos = tile_in_seq * tm + lax.broadcasted_iota(jnp.int32, (tm, 1), 0) + 1
    for g, w in enumerate(POOL_WINDOWS):
        cols = slice(g * POOL_GROUP_DIM, (g + 1) * POOL_GROUP_DIM)
        cur = pbuf_ref[POOL_HALO:POOL_HALO + tm, cols]
        tot = cur
        for j in range(1, w):
            tot = tot + pbuf_ref[POOL_HALO - j:POOL_HALO - j + tm, cols]
        cnt = jnp.minimum(pos, w).astype(F32)
        d_ref[:, cols] = (tot / cnt - cur).astype(BF16)

    ygs = []
    for g in range(len(POOL_WINDOWS)):
        cols = slice(g * POOL_GROUP_DIM, (g + 1) * POOL_GROUP_DIM)
        ygs.append(jnp.dot(d_ref[:, cols], wgrp_ref[g], preferred_element_type=F32))
    yg = jnp.concatenate(ygs, axis=1) * pscale_ref[...]
    y_a = jnp.dot(yg.astype(BF16), wpo_ref[...], preferred_element_type=F32)

    gu = jax.nn.gelu(u_ref[...])
    vn = (_rms_scale(jax.nn.gelu(v_ref[...])) * vgain_ref[...]).astype(BF16)
    row = lax.broadcasted_iota(jnp.int32, (CHUNK, CHUNK), 0)
    col = lax.broadcasted_iota(jnp.int32, (CHUNK, CHUNK), 1)
    causal = col <= row
    n_chunks = tm // CHUNK
    for h in range(SGU_HEADS):
        hcols = slice(h * SGU_HEAD_DIM, (h + 1) * SGU_HEAD_DIM)
        w_h = jnp.where(causal, ws_ref[h], 0.0).astype(BF16)
        v_h = jnp.concatenate(
            [vn[c * CHUNK:(c + 1) * CHUNK, hcols] for c in range(n_chunks)], axis=1)
        s_h = jnp.dot(w_h, v_h, preferred_element_type=F32) + bst_ref[:, h:h + 1]
        for c in range(n_chunks):
            rows = slice(c * CHUNK, (c + 1) * CHUNK)
            sg_ref[rows, hcols] = (gu[rows, hcols] * s_h[:, c * CHUNK:(c + 1) * CHUNK]).astype(BF16)
    y_b = jnp.dot(sg_ref[...], wso_ref[...], preferred_element_type=F32)

    ga = jnp.concatenate([ga0_ref[...], ga1_ref[...]], axis=1)
    gb = jnp.concatenate([gb0_ref[...], gb1_ref[...]], axis=1)
    mix = (jax.nn.sigmoid(ga) * y_a + jax.nn.sigmoid(gb) * y_b).astype(BF16)
    z = jnp.dot(mix, wout_ref[...], preferred_element_type=F32)
    o_ref[...] = x_ref[...] + _rms_scale(z) * gpost_ref[...]


def _mixer(proj, x, w_grp, p_scale, w_po, v_gain, w_s, b_st, w_so, w_out, g_post, layer,
           *, tm, seq):
    m, d = x.shape
    assert seq % tm == 0 and tm % CHUNK == 0 and d == 2 * POOL_WIDTH
    pw = POOL_WIDTH

    def proj_spec(cb):
        return pl.BlockSpec((tm, pw), lambda i: (i, cb))

    def const_spec(shape):
        nd = len(shape)
        return pl.BlockSpec((None,) + shape, lambda i: (layer,) + (0,) * nd,
                            pipeline_mode=pl.Buffered(1))

    kern = functools.partial(_mixer_kernel, tm=tm, tiles_per_seq=seq // tm)
    return pl.pallas_call(
        kern,
        grid=(m // tm,),
        in_specs=[proj_spec(cb) for cb in range(7)] + [
            pl.BlockSpec((tm, d), lambda i: (i, 0)),
            const_spec((len(POOL_WINDOWS), POOL_GROUP_DIM, POOL_GROUP_DIM)),
            const_spec((1, pw)),
            const_spec((pw, d)),
            const_spec((1, SGU_WIDTH)),
            const_spec((SGU_HEADS, CHUNK, CHUNK)),
            const_spec((CHUNK, SGU_HEADS)),
            const_spec((SGU_WIDTH, d)),
            const_spec((d, d)),
            const_spec((1, d)),
        ],
        out_specs=pl.BlockSpec((tm, d), lambda i: (i, 0)),
        out_shape=jax.ShapeDtypeStruct((m, d), F32),
        scratch_shapes=[
            pltpu.VMEM((tm + POOL_HALO, pw), F32),
            pltpu.VMEM((tm, pw), BF16),
            pltpu.VMEM((tm, SGU_WIDTH), BF16),
        ],
        compiler_params=pltpu.CompilerParams(
            dimension_semantics=("arbitrary",),
            vmem_limit_bytes=V7X_VMEM_LIMIT_BYTES),
        name="mixer",
    )(*([proj] * 7), x, w_grp, p_scale, w_po, v_gain, w_s, b_st, w_so, w_out, g_post)


def _pick(total, prefs):
    for t in prefs:
        if total % t == 0:
            return t
    return total


def kernel(x, g_ffn1_pre, w_ffn1_up, w_ffn1_down, g_ffn1_post, g_mix_pre, w_in,
           pool_group_w, pool_scale, w_pool_out, sgu_v_gain, sgu_w_s, sgu_b_s,
           w_sgu_out, w_out, g_mix_post, g_ffn2_pre, w_ffn2_up, w_ffn2_down, g_ffn2_post):
    b, s, d = x.shape
    depth = w_in.shape[0]
    m = b * s
    d_ff = w_ffn1_down.shape[1]

    ffn_tm = _pick(m, (512, 256, 128))
    ffn_tf = _pick(d_ff, (512, 256, 128))
    inp_tm = _pick(m, (1024, 512, 256, 128))
    mix_tm = _pick(s, (256, 128))

    def row(g):
        return g.reshape(g.shape[0], 1, g.shape[1])

    bf = lambda w: w.astype(BF16)
    w1u, w1d, w2u, w2d = bf(w_ffn1_up), bf(w_ffn1_down), bf(w_ffn2_up), bf(w_ffn2_down)
    w_in_b, w_grp, w_po, w_so, w_o = bf(w_in), bf(pool_group_w), bf(w_pool_out), bf(w_sgu_out), bf(w_out)
    b_st = jnp.swapaxes(sgu_b_s, 1, 2)

    xf = x.reshape(m, d)
    for l in range(depth):
        xf = _ffn(xf, row(g_ffn1_pre), w1u, w1d, row(g_ffn1_post), l, tm=ffn_tm, tf=ffn_tf)
        proj = _inproj(xf, row(g_mix_pre), w_in_b, l, tm=inp_tm, tn=POOL_WIDTH)
        xf = _mixer(proj, xf, w_grp, row(pool_scale), w_po, row(sgu_v_gain), sgu_w_s, b_st,
                    w_so, w_o, row(g_mix_post), l, tm=mix_tm, seq=s)
        xf = _ffn(xf, row(g_ffn2_pre), w2u, w2d, row(g_ffn2_post), l, tm=ffn_tm, tf=ffn_tf)
    return xf.reshape(b, s, d)
```

```python
import functools

import jax
import jax.numpy as jnp
from jax import lax
from jax.experimental import pallas as pl
from jax.experimental.pallas import tpu as pltpu

POOL_WINDOWS = (2, 4, 8, 16)
POOL_GROUP_DIM = 256
POOL_WIDTH = POOL_GROUP_DIM * len(POOL_WINDOWS)
POOL_HALO = 16
SGU_HEADS = 8
SGU_HEAD_DIM = 128
SGU_WIDTH = SGU_HEADS * SGU_HEAD_DIM
CHUNK = 128
MACARON_WEIGHT = 0.5
EPS = 1e-6

V7X_VMEM_LIMIT_BYTES = 56 * 1024 * 1024

BF16 = jnp.bfloat16
F32 = jnp.float32


def _rms_scale(x):
    var = jnp.mean(x * x, axis=-1, keepdims=True)
    return x * lax.rsqrt(var + EPS)


N_WEIGHT_SLOTS = 2


def _ffn_chunk_plan(d_ff, prefs):
    for tf in prefs:
        plan = [(o, min(tf, d_ff - o)) for o in range(0, d_ff, tf)]
        if len(plan) % N_WEIGHT_SLOTS == 0 and all(w % 128 == 0 for _, w in plan):
            return plan
    raise ValueError(f"no chunk plan for d_ff={d_ff}")


def _ffn_kernel(x_ref, gpre_ref, wup_hbm, wdn_hbm, gpost_ref, o_ref,
                wg_buf, wu_buf, wd_buf, sem, h_ref, acc_ref, *, layer, plan, d_ff):
    i = pl.program_id(0)
    n_tiles = pl.num_programs(0)
    nk = len(plan)

    def copies(k):
        s = k % N_WEIGHT_SLOTS
        off, w = plan[k]
        return (
            pltpu.make_async_copy(wup_hbm.at[layer, :, pl.ds(off, w)],
                                  wg_buf.at[s, :, pl.ds(0, w)], sem.at[s, 0]),
            pltpu.make_async_copy(wup_hbm.at[layer, :, pl.ds(d_ff + off, w)],
                                  wu_buf.at[s, :, pl.ds(0, w)], sem.at[s, 1]),
            pltpu.make_async_copy(wdn_hbm.at[layer, pl.ds(off, w), :],
                                  wd_buf.at[s, pl.ds(0, w), :], sem.at[s, 2]),
        )

    def start(k):
        for c in copies(k):
            c.start()

    def wait(k):
        for c in copies(k):
            c.wait()

    @pl.when(i == 0)
    def _():
        start(0)

    h_ref[...] = (_rms_scale(x_ref[...]) * gpre_ref[...]).astype(BF16)

    for k in range(nk):
        if k + 1 < nk:
            start(k + 1)
        else:
            @pl.when(i + 1 < n_tiles)
            def _():
                start(0)
        wait(k)
        s = k % N_WEIGHT_SLOTS
        w = plan[k][1]
        h = h_ref[...]
        gate = jnp.dot(h, wg_buf[s, :, 0:w], preferred_element_type=F32)
        up = jnp.dot(h, wu_buf[s, :, 0:w], preferred_element_type=F32)
        a = (gate * jax.nn.sigmoid(gate) * up).astype(BF16)
        part = jnp.dot(a, wd_buf[s, 0:w, :], preferred_element_type=F32)
        if k == 0:
            acc_ref[...] = part
        else:
            acc_ref[...] += part

    f = _rms_scale(acc_ref[...]) * gpost_ref[...]
    o_ref[...] = x_ref[...] + MACARON_WEIGHT * f


def _ffn(x, g_pre, w_up, w_down, g_post, layer, *, tm, tf_prefs):
    m, d = x.shape
    d_ff = w_down.shape[1]
    plan = _ffn_chunk_plan(d_ff, tf_prefs)
    tf = max(w for _, w in plan)
    gain_spec = pl.BlockSpec((None, 1, d), lambda i: (layer, 0, 0))
    kern = functools.partial(_ffn_kernel, layer=layer, plan=plan, d_ff=d_ff)
    return pl.pallas_call(
        kern,
        grid=(m // tm,),
        in_specs=[
            pl.BlockSpec((tm, d), lambda i: (i, 0)),
            gain_spec,
            pl.BlockSpec(memory_space=pl.ANY),
            pl.BlockSpec(memory_space=pl.ANY),
            gain_spec,
        ],
        out_specs=pl.BlockSpec((tm, d), lambda i: (i, 0)),
        out_shape=jax.ShapeDtypeStruct((m, d), F32),
        scratch_shapes=[
            pltpu.VMEM((N_WEIGHT_SLOTS, d, tf), BF16),
            pltpu.VMEM((N_WEIGHT_SLOTS, d, tf), BF16),
            pltpu.VMEM((N_WEIGHT_SLOTS, tf, d), BF16),
            pltpu.SemaphoreType.DMA((N_WEIGHT_SLOTS, 3)),
            pltpu.VMEM((tm, d), BF16),
            pltpu.VMEM((tm, d), F32),
        ],
        compiler_params=pltpu.CompilerParams(
            dimension_semantics=("arbitrary",),
            vmem_limit_bytes=V7X_VMEM_LIMIT_BYTES),
        name="ffn",
    )(x, g_pre, w_up, w_down, g_post)


def _inproj_kernel(x_ref, g_ref, w_ref, o_ref, h_ref):
    @pl.when(pl.program_id(1) == 0)
    def _():
        h_ref[...] = (_rms_scale(x_ref[...]) * g_ref[...]).astype(BF16)

    o_ref[...] = jnp.dot(h_ref[...], w_ref[...], preferred_element_type=F32)


def _inproj(x, g, w_in, layer, *, tm, tn):
    m, d = x.shape
    n = w_in.shape[2]
    return pl.pallas_call(
        _inproj_kernel,
        grid=(m // tm, n // tn),
        in_specs=[
            pl.BlockSpec((tm, d), lambda i, j: (i, 0)),
            pl.BlockSpec((None, 1, d), lambda i, j: (layer, 0, 0)),
            pl.BlockSpec((None, d, tn), lambda i, j: (layer, 0, j)),
        ],
        out_specs=pl.BlockSpec((tm, tn), lambda i, j: (i, j)),
        out_shape=jax.ShapeDtypeStruct((m, n), F32),
        scratch_shapes=[pltpu.VMEM((tm, d), BF16)],
        compiler_params=pltpu.CompilerParams(
            dimension_semantics=("arbitrary", "arbitrary"),
            vmem_limit_bytes=V7X_VMEM_LIMIT_BYTES),
        name="in_proj",
    )(x, g, w_in)


def _mixer_kernel(p_ref, u_ref, v_ref, ga0_ref, ga1_ref, gb0_ref, gb1_ref, x_ref,
                  wgrp_ref, pscale_ref, wpo_ref, vgain_ref, ws_ref, bst_ref,
                  wso_ref, wout_ref, gpost_ref, o_ref, pbuf_ref, d_ref, sg_ref,
                  *, tm, tiles_per_seq):
    i = pl.program_id(0)
    tile_in_seq = i % tiles_per_seq

    @pl.when(tile_in_seq == 0)
    def _():
        pbuf_ref[0:POOL_HALO, :] = jnp.zeros((POOL_HALO, POOL_WIDTH), F32)

    @pl.when(tile_in_seq != 0)
    def _():
        pbuf_ref[0:POOL_HALO, :] = pbuf_ref[tm:tm + POOL_HALO, :]

    pbuf_ref[POOL_HALO:POOL_HALO + tm, :] = p_ref[...]

    pos = tile_in_seq * tm + lax.broadcasted_iota(jnp.int32, (tm, 1), 0) + 1
    for g, w in enumerate(POOL_WINDOWS):
        cols = slice(g * POOL_GROUP_DIM, (g + 1) * POOL_GROUP_DIM)
        cur = pbuf_ref[POOL_HALO:POOL_HALO + tm, cols]
        tot = cur
        for j in range(1, w):
            tot = tot + pbuf_ref[POOL_HALO - j:POOL_HALO - j + tm, cols]
        cnt = jnp.minimum(pos, w).astype(F32)
        d_ref[:, cols] = (tot / cnt - cur).astype(BF16)

    ygs = []
    for g in range(len(POOL_WINDOWS)):
        cols = slice(g * POOL_GROUP_DIM, (g + 1) * POOL_GROUP_DIM)
        ygs.append(jnp.dot(d_ref[:, cols], wgrp_ref[g], preferred_element_type=F32))
    yg = jnp.concatenate(ygs, axis=1) * pscale_ref[...]
    y_a = jnp.dot(yg.astype(BF16), wpo_ref[...], preferred_element_type=F32)

    gu = jax.nn.gelu(u_ref[...])
    vn = (_rms_scale(jax.nn.gelu(v_ref[...])) * vgain_ref[...]).astype(BF16)
    row = lax.broadcasted_iota(jnp.int32, (CHUNK, CHUNK), 0)
    col = lax.broadcasted_iota(jnp.int32, (CHUNK, CHUNK), 1)
    causal = col <= row
    n_chunks = tm // CHUNK
    for h in range(SGU_HEADS):
        hcols = slice(h * SGU_HEAD_DIM, (h + 1) * SGU_HEAD_DIM)
        w_h = jnp.where(causal, ws_ref[h], 0.0).astype(BF16)
        v_h = jnp.concatenate(
            [vn[c * CHUNK:(c + 1) * CHUNK, hcols] for c in range(n_chunks)], axis=1)
        s_h = jnp.dot(w_h, v_h, preferred_element_type=F32) + bst_ref[:, h:h + 1]
        for c in range(n_chunks):
            rows = slice(c * CHUNK, (c + 1) * CHUNK)
            sg_ref[rows, hcols] = (gu[rows, hcols] * s_h[:, c * CHUNK:(c + 1) * CHUNK]).astype(BF16)
    y_b = jnp.dot(sg_ref[...], wso_ref[...], preferred_element_type=F32)

    ga = jnp.concatenate([ga0_ref[...], ga1_ref[...]], axis=1)
    gb = jnp.concatenate([gb0_ref[...], gb1_ref[...]], axis=1)
    mix = (jax.nn.sigmoid(ga) * y_a + jax.nn.sigmoid(gb) * y_b).astype(BF16)
    z = jnp.dot(mix, wout_ref[...], preferred_element_type=F32)
    o_ref[...] = x_ref[...] + _rms_scale(z) * gpost_ref[...]


def _mixer(proj, x, w_grp, p_scale, w_po, v_gain, w_s, b_st, w_so, w_out, g_post, layer,
           *, tm, seq):
    m, d = x.shape
    assert seq % tm == 0 and tm % CHUNK == 0 and d == 2 * POOL_WIDTH
    pw = POOL_WIDTH

    def proj_spec(cb):
        return pl.BlockSpec((tm, pw), lambda i: (i, cb))

    def const_spec(shape):
        nd = len(shape)
        return pl.BlockSpec((None,) + shape, lambda i: (layer,) + (0,) * nd,
                            pipeline_mode=pl.Buffered(1))

    kern = functools.partial(_mixer_kernel, tm=tm, tiles_per_seq=seq // tm)
    return pl.pallas_call(
        kern,
        grid=(m // tm,),
        in_specs=[proj_spec(cb) for cb in range(7)] + [
            pl.BlockSpec((tm, d), lambda i: (i, 0)),
            const_spec((len(POOL_WINDOWS), POOL_GROUP_DIM, POOL_GROUP_DIM)),
            const_spec((1, pw)),
            const_spec((pw, d)),
            const_spec((1, SGU_WIDTH)),
            const_spec((SGU_HEADS, CHUNK, CHUNK)),
            const_spec((CHUNK, SGU_HEADS)),
            const_spec((SGU_WIDTH, d)),
            const_spec((d, d)),
            const_spec((1, d)),
        ],
        out_specs=pl.BlockSpec((tm, d), lambda i: (i, 0)),
        out_shape=jax.ShapeDtypeStruct((m, d), F32),
        scratch_shapes=[
            pltpu.VMEM((tm + POOL_HALO, pw), F32),
            pltpu.VMEM((tm, pw), BF16),
            pltpu.VMEM((tm, SGU_WIDTH), BF16),
        ],
        compiler_params=pltpu.CompilerParams(
            dimension_semantics=("arbitrary",),
            vmem_limit_bytes=V7X_VMEM_LIMIT_BYTES),
        name="mixer",
    )(*([proj] * 7), x, w_grp, p_scale, w_po, v_gain, w_s, b_st, w_so, w_out, g_post)


def _pick(total, prefs):
    for t in prefs:
        if total % t == 0:
            return t
    return total


def kernel(x, g_ffn1_pre, w_ffn1_up, w_ffn1_down, g_ffn1_post, g_mix_pre, w_in,
           pool_group_w, pool_scale, w_pool_out, sgu_v_gain, sgu_w_s, sgu_b_s,
           w_sgu_out, w_out, g_mix_post, g_ffn2_pre, w_ffn2_up, w_ffn2_down, g_ffn2_post):
    b, s, d = x.shape
    depth = w_in.shape[0]
    m = b * s

    ffn_tm = _pick(m, (512, 256, 128))
    ffn_tf = (1024, 512, 256, 128)
    inp_tm = _pick(m, (1024, 512, 256, 128))
    mix_tm = _pick(s, (256, 128))

    def row(g):
        return g.reshape(g.shape[0], 1, g.shape[1])

    bf = lambda w: w.astype(BF16)
    w1u, w1d, w2u, w2d = bf(w_ffn1_up), bf(w_ffn1_down), bf(w_ffn2_up), bf(w_ffn2_down)
    w_in_b, w_grp, w_po, w_so, w_o = bf(w_in), bf(pool_group_w), bf(w_pool_out), bf(w_sgu_out), bf(w_out)
    b_st = jnp.swapaxes(sgu_b_s, 1, 2)

    xf = x.reshape(m, d)
    for l in range(depth):
        xf = _ffn(xf, row(g_ffn1_pre), w1u, w1d, row(g_ffn1_post), l, tm=ffn_tm, tf_prefs=ffn_tf)
        proj = _inproj(xf, row(g_mix_pre), w_in_b, l, tm=inp_tm, tn=POOL_WIDTH)
        xf = _mixer(proj, xf, w_grp, row(pool_scale), w_po, row(sgu_v_gain), sgu_w_s, b_st,
                    w_so, w_o, row(g_mix_post), l, tm=mix_tm, seq=s)
        xf = _ffn(xf, row(g_ffn2_pre), w2u, w2d, row(g_ffn2_post), l, tm=ffn_tm, tf_prefs=ffn_tf)
    return xf.reshape(b, s, d)
```

```python
import functools

import jax
import jax.numpy as jnp
from jax import lax
from jax.experimental import pallas as pl
from jax.experimental.pallas import tpu as pltpu

POOL_WINDOWS = (2, 4, 8, 16)
POOL_GROUP_DIM = 256
POOL_WIDTH = POOL_GROUP_DIM * len(POOL_WINDOWS)
POOL_HALO = 16
POOL_T0 = 32
assert POOL_WINDOWS == (2, 4, 8, 16) and POOL_T0 >= 2 * POOL_HALO
SGU_HEADS = 8
SGU_HEAD_DIM = 128
SGU_WIDTH = SGU_HEADS * SGU_HEAD_DIM
CHUNK = 128
MACARON_WEIGHT = 0.5
EPS = 1e-6

V7X_VMEM_LIMIT_BYTES = 56 * 1024 * 1024

BF16 = jnp.bfloat16
F32 = jnp.float32


def _rms_scale(x):
    var = jnp.mean(x * x, axis=-1, keepdims=True)
    return x * lax.rsqrt(var + EPS)


N_WEIGHT_SLOTS = 2


def _ffn_chunk_plan(d_ff, prefs):
    for tf in prefs:
        plan = [(o, min(tf, d_ff - o)) for o in range(0, d_ff, tf)]
        if len(plan) % N_WEIGHT_SLOTS == 0 and all(w % 128 == 0 for _, w in plan):
            return plan
    raise ValueError(f"no chunk plan for d_ff={d_ff}")


def _ffn_kernel(x_ref, gpre_ref, wup_hbm, wdn_hbm, gpost_ref, o_ref,
                wg_buf, wu_buf, wd_buf, sem, h_ref, acc_ref, *, layer, plan, d_ff):
    i = pl.program_id(0)
    n_tiles = pl.num_programs(0)
    nk = len(plan)

    def copies(k):
        s = k % N_WEIGHT_SLOTS
        off, w = plan[k]
        return (
            pltpu.make_async_copy(wup_hbm.at[layer, :, pl.ds(off, w)],
                                  wg_buf.at[s, :, pl.ds(0, w)], sem.at[s, 0]),
            pltpu.make_async_copy(wup_hbm.at[layer, :, pl.ds(d_ff + off, w)],
                                  wu_buf.at[s, :, pl.ds(0, w)], sem.at[s, 1]),
            pltpu.make_async_copy(wdn_hbm.at[layer, pl.ds(off, w), :],
                                  wd_buf.at[s, pl.ds(0, w), :], sem.at[s, 2]),
        )

    def start(k):
        for c in copies(k):
            c.start()

    def wait(k):
        for c in copies(k):
            c.wait()

    @pl.when(i == 0)
    def _():
        start(0)

    h_ref[...] = (_rms_scale(x_ref[...]) * gpre_ref[...]).astype(BF16)

    for k in range(nk):
        if k + 1 < nk:
            start(k + 1)
        else:
            @pl.when(i + 1 < n_tiles)
            def _():
                start(0)
        wait(k)
        s = k % N_WEIGHT_SLOTS
        w = plan[k][1]
        h = h_ref[...]
        gate = jnp.dot(h, wg_buf[s, :, 0:w], preferred_element_type=F32)
        up = jnp.dot(h, wu_buf[s, :, 0:w], preferred_element_type=F32)
        a = (gate * jax.nn.sigmoid(gate) * up).astype(BF16)
        part = jnp.dot(a, wd_buf[s, 0:w, :], preferred_element_type=F32)
        if k == 0:
            acc_ref[...] = part
        else:
            acc_ref[...] += part

    o_ref[...] = x_ref[...] + _rms_scale(acc_ref[...]) * (gpost_ref[...] * MACARON_WEIGHT)


def _ffn(x, g_pre, w_up, w_down, g_post, layer, *, tm, tf_prefs):
    m, d = x.shape
    d_ff = w_down.shape[1]
    plan = _ffn_chunk_plan(d_ff, tf_prefs)
    tf = max(w for _, w in plan)
    gain_spec = pl.BlockSpec((None, 1, d), lambda i: (layer, 0, 0))
    kern = functools.partial(_ffn_kernel, layer=layer, plan=plan, d_ff=d_ff)
    return pl.pallas_call(
        kern,
        grid=(m // tm,),
        in_specs=[
            pl.BlockSpec((tm, d), lambda i: (i, 0)),
            gain_spec,
            pl.BlockSpec(memory_space=pl.ANY),
            pl.BlockSpec(memory_space=pl.ANY),
            gain_spec,
        ],
        out_specs=pl.BlockSpec((tm, d), lambda i: (i, 0)),
        out_shape=jax.ShapeDtypeStruct((m, d), F32),
        scratch_shapes=[
            pltpu.VMEM((N_WEIGHT_SLOTS, d, tf), BF16),
            pltpu.VMEM((N_WEIGHT_SLOTS, d, tf), BF16),
            pltpu.VMEM((N_WEIGHT_SLOTS, tf, d), BF16),
            pltpu.SemaphoreType.DMA((N_WEIGHT_SLOTS, 3)),
            pltpu.VMEM((tm, d), BF16),
            pltpu.VMEM((tm, d), F32),
        ],
        compiler_params=pltpu.CompilerParams(
            dimension_semantics=("arbitrary",),
            vmem_limit_bytes=V7X_VMEM_LIMIT_BYTES),
        name="ffn",
    )(x, g_pre, w_up, w_down, g_post)


def _cyclic_slots(n_chunks):
    slots = [k % 2 for k in range(n_chunks)]
    if n_chunks % 2:
        slots[-1] = 2
    return slots


def _inproj_kernel(x_ref, g_ref, w_hbm, o_ref, w_buf, sem, h_ref, *, layer, tn, slots):
    i = pl.program_id(0)
    n_tiles = pl.num_programs(0)
    nk = len(slots)

    def copy(k):
        return pltpu.make_async_copy(w_hbm.at[layer, :, pl.ds(k * tn, tn)], w_buf.at[slots[k]],
                                     sem.at[slots[k]])

    @pl.when(i == 0)
    def _():
        copy(0).start()

    h_ref[...] = (_rms_scale(x_ref[...]) * g_ref[...]).astype(BF16)

    for k in range(nk):
        if k + 1 < nk:
            copy(k + 1).start()
        else:
            @pl.when(i + 1 < n_tiles)
            def _():
                copy(0).start()
        copy(k).wait()
        o_ref[:, k * tn:(k + 1) * tn] = jnp.dot(h_ref[...], w_buf[slots[k]],
                                                preferred_element_type=F32)


def _inproj(x, g, w_in, layer, *, tm, tn):
    m, d = x.shape
    n = w_in.shape[2]
    assert n % tn == 0 and n // tn > 1
    slots = _cyclic_slots(n // tn)
    kern = functools.partial(_inproj_kernel, layer=layer, tn=tn, slots=slots)
    return pl.pallas_call(
        kern,
        grid=(m // tm,),
        in_specs=[
            pl.BlockSpec((tm, d), lambda i: (i, 0)),
            pl.BlockSpec((None, 1, d), lambda i: (layer, 0, 0)),
            pl.BlockSpec(memory_space=pl.ANY),
        ],
        out_specs=pl.BlockSpec((tm, n), lambda i: (i, 0)),
        out_shape=jax.ShapeDtypeStruct((m, n), F32),
        scratch_shapes=[
            pltpu.VMEM((max(slots) + 1, d, tn), BF16),
            pltpu.SemaphoreType.DMA((max(slots) + 1,)),
            pltpu.VMEM((tm, d), BF16),
        ],
        compiler_params=pltpu.CompilerParams(
            dimension_semantics=("arbitrary",),
            vmem_limit_bytes=V7X_VMEM_LIMIT_BYTES),
        name="in_proj",
    )(x, g, w_in)


def _mixer_kernel(p_ref, u_ref, v_ref, ga0_ref, ga1_ref, gb0_ref, gb1_ref, x_ref,
                  wgrp_ref, pscale_ref, wpo_ref, vgain_ref, ws_ref, bst_ref,
                  wso_ref, wout_ref, gpost_ref, o_ref, pbuf_ref, s2_ref, s4_ref, s8_ref, d_ref, sg_ref,
                  *, tm, tiles_per_seq):
    i = pl.program_id(0)
    tile_in_seq = i % tiles_per_seq

    t0 = POOL_T0
    end = t0 + tm
    gd = POOL_GROUP_DIM

    @pl.when(tile_in_seq == 0)
    def _():
        pbuf_ref[0:t0, :] = jnp.zeros((t0, POOL_WIDTH), F32)

    @pl.when(tile_in_seq != 0)
    def _():
        pbuf_ref[t0 - POOL_HALO:t0, :] = pbuf_ref[end - POOL_HALO:end, :]

    pbuf_ref[t0:end, :] = p_ref[...]

    s2_ref[8:end, :] = pbuf_ref[8:end, :] + pbuf_ref[7:end - 1, :]
    s4_ref[16:end, :] = s2_ref[16:end, gd:] + s2_ref[14:end - 2, gd:]
    s8_ref[24:end, :] = s4_ref[24:end, gd:] + s4_ref[20:end - 4, gd:]
    s16 = s8_ref[t0:end, gd:] + s8_ref[t0 - 8:end - 8, gd:]
    window_sums = (s2_ref[t0:end, 0:gd], s4_ref[t0:end, 0:gd], s8_ref[t0:end, 0:gd], s16)
    pos = tile_in_seq * tm + lax.broadcasted_iota(jnp.int32, (tm, 1), 0) + 1
    for g, w in enumerate(POOL_WINDOWS):
        cols = slice(g * gd, (g + 1) * gd)
        cnt = jnp.minimum(pos, w).astype(F32)
        d_ref[:, cols] = (window_sums[g] / cnt - pbuf_ref[t0:end, cols]).astype(BF16)

    ygs = []
    for g in range(len(POOL_WINDOWS)):
        cols = slice(g * POOL_GROUP_DIM, (g + 1) * POOL_GROUP_DIM)
        ygs.append(jnp.dot(d_ref[:, cols], wgrp_ref[g], preferred_element_type=F32))
    yg = jnp.concatenate(ygs, axis=1) * pscale_ref[...]
    y_a = jnp.dot(yg.astype(BF16), wpo_ref[...], preferred_element_type=F32)

    gu = jax.nn.gelu(u_ref[...])
    vn = (_rms_scale(jax.nn.gelu(v_ref[...])) * vgain_ref[...]).astype(BF16)
    row = lax.broadcasted_iota(jnp.int32, (CHUNK, CHUNK), 0)
    col = lax.broadcasted_iota(jnp.int32, (CHUNK, CHUNK), 1)
    causal = col <= row
    n_chunks = tm // CHUNK
    for h in range(SGU_HEADS):
        hcols = slice(h * SGU_HEAD_DIM, (h + 1) * SGU_HEAD_DIM)
        w_h = jnp.where(causal, ws_ref[h], 0.0).astype(BF16)
        v_h = jnp.concatenate(
            [vn[c * CHUNK:(c + 1) * CHUNK, hcols] for c in range(n_chunks)], axis=1)
        s_h = jnp.dot(w_h, v_h, preferred_element_type=F32) + bst_ref[:, h:h + 1]
        for c in range(n_chunks):
            rows = slice(c * CHUNK, (c + 1) * CHUNK)
            sg_ref[rows, hcols] = (gu[rows, hcols] * s_h[:, c * CHUNK:(c + 1) * CHUNK]).astype(BF16)
    y_b = jnp.dot(sg_ref[...], wso_ref[...], preferred_element_type=F32)

    ga = jnp.concatenate([ga0_ref[...], ga1_ref[...]], axis=1)
    gb = jnp.concatenate([gb0_ref[...], gb1_ref[...]], axis=1)
    mix = (jax.nn.sigmoid(ga) * y_a + jax.nn.sigmoid(gb) * y_b).astype(BF16)
    z = jnp.dot(mix, wout_ref[...], preferred_element_type=F32)
    o_ref[...] = x_ref[...] + _rms_scale(z) * gpost_ref[...]


def _mixer(proj, x, w_grp, p_scale, w_po, v_gain, w_s, b_st, w_so, w_out, g_post, layer,
           *, tm, seq):
    m, d = x.shape
    assert seq % tm == 0 and tm % CHUNK == 0 and d == 2 * POOL_WIDTH
    pw = POOL_WIDTH

    def proj_spec(cb):
        return pl.BlockSpec((tm, pw), lambda i: (i, cb))

    def const_spec(shape):
        nd = len(shape)
        return pl.BlockSpec((None,) + shape, lambda i: (layer,) + (0,) * nd,
                            pipeline_mode=pl.Buffered(1))

    kern = functools.partial(_mixer_kernel, tm=tm, tiles_per_seq=seq // tm)
    return pl.pallas_call(
        kern,
        grid=(m // tm,),
        in_specs=[proj_spec(cb) for cb in range(7)] + [
            pl.BlockSpec((tm, d), lambda i: (i, 0)),
            const_spec((len(POOL_WINDOWS), POOL_GROUP_DIM, POOL_GROUP_DIM)),
            const_spec((1, pw)),
            const_spec((pw, d)),
            const_spec((1, SGU_WIDTH)),
            const_spec((SGU_HEADS, CHUNK, CHUNK)),
            const_spec((CHUNK, SGU_HEADS)),
            const_spec((SGU_WIDTH, d)),
            const_spec((d, d)),
            const_spec((1, d)),
        ],
        out_specs=pl.BlockSpec((tm, d), lambda i: (i, 0)),
        out_shape=jax.ShapeDtypeStruct((m, d), F32),
        scratch_shapes=[
            pltpu.VMEM((tm + POOL_T0, pw), F32),
            pltpu.VMEM((tm + POOL_T0, pw), F32),
            pltpu.VMEM((tm + POOL_T0, pw - POOL_GROUP_DIM), F32),
            pltpu.VMEM((tm + POOL_T0, pw - 2 * POOL_GROUP_DIM), F32),
            pltpu.VMEM((tm, pw), BF16),
            pltpu.VMEM((tm, SGU_WIDTH), BF16),
        ],
        compiler_params=pltpu.CompilerParams(
            dimension_semantics=("arbitrary",),
            vmem_limit_bytes=V7X_VMEM_LIMIT_BYTES),
        name="mixer",
    )(*([proj] * 7), x, w_grp, p_scale, w_po, v_gain, w_s, b_st, w_so, w_out, g_post)


def _pick(total, prefs):
    for t in prefs:
        if total % t == 0:
            return t
    return total


def kernel(x, g_ffn1_pre, w_ffn1_up, w_ffn1_down, g_ffn1_post, g_mix_pre, w_in,
           pool_group_w, pool_scale, w_pool_out, sgu_v_gain, sgu_w_s, sgu_b_s,
           w_sgu_out, w_out, g_mix_post, g_ffn2_pre, w_ffn2_up, w_ffn2_down, g_ffn2_post):
    b, s, d = x.shape
    depth = w_in.shape[0]
    m = b * s

    ffn_tm = _pick(m, (512, 256, 128))
    ffn_tf = (1024, 512, 256, 128)
    inp_tm = _pick(m, (512, 256, 128))
    mix_tm = _pick(s, (256, 128))

    def row(g):
        return g.reshape(g.shape[0], 1, g.shape[1])

    bf = lambda w: w.astype(BF16)
    w1u, w1d, w2u, w2d = bf(w_ffn1_up), bf(w_ffn1_down), bf(w_ffn2_up), bf(w_ffn2_down)
    w_in_b, w_grp, w_po, w_so, w_o = bf(w_in), bf(pool_group_w), bf(w_pool_out), bf(w_sgu_out), bf(w_out)
    b_st = jnp.swapaxes(sgu_b_s, 1, 2)

    xf = x.reshape(m, d)
    for l in range(depth):
        xf = _ffn(xf, row(g_ffn1_pre), w1u, w1d, row(g_ffn1_post), l, tm=ffn_tm, tf_prefs=ffn_tf)
        proj = _inproj(xf, row(g_mix_pre), w_in_b, l, tm=inp_tm, tn=POOL_WIDTH)
        xf = _mixer(proj, xf, w_grp, row(pool_scale), w_po, row(sgu_v_gain), sgu_w_s, b_st,
                    w_so, w_o, row(g_mix_post), l, tm=mix_tm, seq=s)
        xf = _ffn(xf, row(g_ffn2_pre), w2u, w2d, row(g_ffn2_post), l, tm=ffn_tm, tf_prefs=ffn_tf)
    return xf.reshape(b, s, d)
```
